```python
import math
import jax, jax.numpy as jnp
from jax import lax
import numpy as np

D_MODEL = 2048
BATCH = 4
SEQ = 2048
DEPTH = 1
DEC_BATCH = 128
DEC_SEQ = 1
PAST_LEN = 2048
PAGE_SIZE = 128

H_DIFF = 8
HD_DIFF = 64
VD_DIFF = 2 * HD_DIFF
W_DIFF = H_DIFF * VD_DIFF
H_SB = 16
HD_SB = 64
W_SB = H_SB * HD_SB
MIX_WIDTH = W_DIFF + W_SB
QKV_WIDTH = 3 * W_DIFF + 3 * W_SB
ROPE_THETA = 500000.0
ROT_DIM = HD_DIFF // 4
N_GROUPS = 4
EXPERTS_PER_GROUP = 8
N_EXPERTS = N_GROUPS * EXPERTS_PER_GROUP
TOP_K_INNER = 2
D_EXPERT = 512
Q_BLOCK = 128
EPS = 1e-6
DIFF_SCALE = HD_DIFF ** -0.5
SB_SCALE = HD_SB ** -0.5

kernel_name = "hymba_diff_stickbreak_hmoe_step"


def _rms_norm(x, g):
    xf = x.astype(jnp.float32)
    y = xf * lax.rsqrt(jnp.mean(xf * xf, axis=-1, keepdims=True) + EPS)
    return (y * g.astype(jnp.float32)).astype(x.dtype)


def _partial_rope(x, pos):
    half = ROT_DIM // 2
    inv_freq = jnp.power(jnp.float32(ROPE_THETA), -jnp.arange(half, dtype=jnp.float32) * (2.0 / ROT_DIM))
    ang = pos.astype(jnp.float32)[:, None] * inv_freq[None, :]
    cos = jnp.cos(ang)[None, :, None, None, :]
    sin = jnp.sin(ang)[None, :, None, None, :]
    xr = x[..., :ROT_DIM].astype(jnp.float32)
    x1, x2 = xr[..., :half], xr[..., half:]
    rot = jnp.concatenate([x1 * cos - x2 * sin, x2 * cos + x1 * sin], axis=-1).astype(x.dtype)
    return jnp.concatenate([rot, x[..., ROT_DIM:]], axis=-1)


def _diff_weights(s, mask, lam):
    s = jnp.where(mask, s, -jnp.inf)
    p = jax.nn.softmax(s, axis=-1)
    return p[:, :, 0] - lam * p[:, :, 1]


def _stick_break_weights(z, mask):
    log_beta = jax.nn.log_sigmoid(z)
    log_om = jnp.where(mask, jax.nn.log_sigmoid(-z), 0.0)
    after = lax.cumsum(log_om, axis=z.ndim - 1, reverse=True) - log_om
    return jnp.where(mask, jnp.exp(log_beta + after), 0.0)


def _attend_prompt(qd, kd, vd, qs, ks, vs, lam):
    b, s = qd.shape[0], qd.shape[1]
    nb = s // Q_BLOCK
    kpos = jnp.arange(s)

    def blocks(t):
        return jnp.moveaxis(t.reshape((b, nb, Q_BLOCK) + t.shape[2:]), 1, 0)

    def one_block(args):
        qd_b, qs_b, qpos = args
        sd = jnp.einsum('bqhcd,bkhcd->bhcqk', qd_b, kd, preferred_element_type=jnp.float32) * DIFF_SCALE
        a = _diff_weights(sd, kpos[None, :] <= qpos[:, None], lam)
        od = jnp.einsum('bhqk,bkhe->bqhe', a.astype(vd.dtype), vd)
        z = jnp.einsum('bqhd,bkhd->bhqk', qs_b, ks, preferred_element_type=jnp.float32) * SB_SCALE
        w = _stick_break_weights(z, kpos[None, :] < qpos[:, None])
        osb = jnp.einsum('bhqk,bkhd->bqhd', w.astype(vs.dtype), vs)
        return od, osb

    od, osb = lax.map(one_block, (blocks(qd), blocks(qs), kpos.reshape(nb, Q_BLOCK)))

    def unblock(t):
        return jnp.moveaxis(t, 0, 1).reshape((b, s) + t.shape[3:])

    return unblock(od), unblock(osb)


def _attend_sample(qd, kd, vd, qs, ks, vs, lam, kd_p, vd_p, ks_p, vs_p):
    n_new = qd.shape[1]
    past = kd_p.shape[1]
    qpos = past + jnp.arange(n_new)
    kpos = jnp.arange(past + n_new)
    sd = jnp.concatenate([
        jnp.einsum('bqhcd,bkhcd->bhcqk', qd, kd_p, preferred_element_type=jnp.float32),
        jnp.einsum('bqhcd,bkhcd->bhcqk', qd, kd, preferred_element_type=jnp.float32)], axis=-1) * DIFF_SCALE
    a = _diff_weights(sd, kpos[None, :] <= qpos[:, None], lam).astype(vd.dtype)
    od = (jnp.einsum('bhqk,bkhe->bqhe', a[..., :past], vd_p)
          + jnp.einsum('bhqk,bkhe->bqhe', a[..., past:], vd))
    z = jnp.concatenate([
        jnp.einsum('bqhd,bkhd->bhqk', qs, ks_p, preferred_element_type=jnp.float32),
        jnp.einsum('bqhd,bkhd->bhqk', qs, ks, preferred_element_type=jnp.float32)], axis=-1) * SB_SCALE
    w = _stick_break_weights(z, kpos[None, :] < qpos[:, None]).astype(vs.dtype)
    osb = (jnp.einsum('bhqk,bkhd->bqhd', w[..., :past], vs_p)
           + jnp.einsum('bhqk,bkhd->bqhd', w[..., past:], vs))
    return od, osb


def _hier_moe(h, w_rg, b_rg, w_re, b_re, w_gate, w_up, w_down):
    b, s, d = h.shape
    hf = h.reshape(b * s, d)
    lg = jnp.einsum('td,dg->tg', hf, w_rg, preferred_element_type=jnp.float32) + b_rg.astype(jnp.float32)
    pg = jax.nn.softmax(lg, axis=-1)
    g_idx = jnp.argmax(lg, axis=-1)
    g_w = jnp.take_along_axis(pg, g_idx[:, None], axis=1)[:, 0]
    le = jnp.einsum('td,dge->tge', hf, w_re, preferred_element_type=jnp.float32) + b_re.astype(jnp.float32)
    le_sel = jnp.take_along_axis(le, g_idx[:, None, None], axis=1)[:, 0]
    tv, ti = lax.top_k(le_sel, TOP_K_INNER)
    tw = jax.nn.softmax(tv, axis=-1)
    inner = jnp.sum(jax.nn.one_hot(ti, EXPERTS_PER_GROUP, dtype=jnp.float32) * tw[..., None], axis=1)
    comb = (jax.nn.one_hot(g_idx, N_GROUPS, dtype=jnp.float32)[:, :, None]
            * inner[:, None, :] * g_w[:, None, None]).reshape(b * s, N_EXPERTS).astype(h.dtype)
    out = jnp.zeros_like(hf)
    for e in range(N_EXPERTS):
        act = jax.nn.silu(hf @ w_gate[e]) * (hf @ w_up[e])
        out = out + comb[:, e:e + 1] * (act @ w_down[e])
    return out.reshape(b, s, d)


def _layer(x, c, pos, attend, lp, lam_init):
    b, s = x.shape[0], x.shape[1]
    mod = (jax.nn.silu(c) @ lp['w_ada'] + lp['b_ada']).reshape(c.shape[0], 6, 1, D_MODEL)
    sh1, sc1, g1, sh2, sc2, g2 = (mod[:, 0], mod[:, 1], mod[:, 2], mod[:, 3], mod[:, 4], mod[:, 5])

    h = _rms_norm(x, lp['g_pre1']) * (1 + sc1) + sh1
    qkv = h @ lp['w_qkv']
    o1 = W_DIFF; o2 = o1 + W_DIFF; o3 = o2 + W_DIFF; o4 = o3 + W_SB; o5 = o4 + W_SB
    qd = _partial_rope(qkv[..., :o1].reshape(b, s, H_DIFF, 2, HD_DIFF), pos)
    kd = _partial_rope(qkv[..., o1:o2].reshape(b, s, H_DIFF, 2, HD_DIFF), pos)
    vd = qkv[..., o2:o3].reshape(b, s, H_DIFF, VD_DIFF)
    qs = qkv[..., o3:o4].reshape(b, s, H_SB, HD_SB)
    ks = qkv[..., o4:o5].reshape(b, s, H_SB, HD_SB)
    vs = qkv[..., o5:].reshape(b, s, H_SB, HD_SB)
    f32 = jnp.float32
    lam = (jnp.exp(jnp.sum(lp['lq1'].astype(f32) * lp['lk1'].astype(f32)))
           - jnp.exp(jnp.sum(lp['lq2'].astype(f32) * lp['lk2'].astype(f32))) + lam_init)
    od, osb = attend(qd, kd, vd, qs, ks, vs, lam)
    od = _rms_norm(od, lp['g_subln']) * (1.0 - lam_init)
    mixed = jnp.concatenate([od.reshape(b, s, W_DIFF), osb.reshape(b, s, W_SB).astype(od.dtype)], axis=-1)
    x = x + g1 * _rms_norm(mixed @ lp['w_o'], lp['g_post1'])

    h2 = _rms_norm(x, lp['g_pre2']) * (1 + sc2) + sh2
    f = _hier_moe(h2, lp['w_rg'], lp['b_rg'], lp['w_re'], lp['b_re'], lp['w_gate'], lp['w_up'], lp['w_down'])
    x = x + g2 * _rms_norm(f, lp['g_post2'])
    return x, (kd, vd, ks, vs)


def setup_inputs(seed: int = 0) -> dict:
    key = jax.random.key(seed)
    ks_ = jax.random.split(key, 32)
    f32 = jnp.float32
    n_pages = PAST_LEN // PAGE_SIZE
    n_used = DEC_BATCH * n_pages
    n_pool = n_used + max(1, n_used // 4)
    nrm = lambda k, shape, sc: jax.random.normal(k, shape, f32) * sc
    gain = lambda k, shape: 1.0 + 0.02 * jax.random.normal(k, shape, f32)
    page_table = jax.random.permutation(ks_[8], n_pool)[:n_used].reshape(DEC_BATCH, n_pages).astype(jnp.int32)
    return {
        "x_prompt": nrm(ks_[0], (BATCH, SEQ, D_MODEL), 1.0),
        "x_sample": nrm(ks_[1], (DEC_BATCH, DEC_SEQ, D_MODEL), 1.0),
        "c_prompt": nrm(ks_[2], (BATCH, D_MODEL), 1.0),
        "c_sample": nrm(ks_[3], (DEC_BATCH, D_MODEL), 1.0),
        "cache_diff_k": nrm(ks_[4], (DEPTH, n_pool, PAGE_SIZE, H_DIFF, 2, HD_DIFF), 1.0),
        "cache_diff_v": nrm(ks_[5], (DEPTH, n_pool, PAGE_SIZE, H_DIFF, VD_DIFF), 1.0),
        "cache_sb_k": nrm(ks_[6], (DEPTH, n_pool, PAGE_SIZE, H_SB, HD_SB), 1.0),
        "cache_sb_v": nrm(ks_[7], (DEPTH, n_pool, PAGE_SIZE, H_SB, HD_SB), 1.0),
        "page_table": page_table,
        "w_ada": nrm(ks_[9], (DEPTH, D_MODEL, 6 * D_MODEL), 0.5 * D_MODEL ** -0.5),
        "b_ada": nrm(ks_[10], (DEPTH, 6 * D_MODEL), 0.02),
        "g_pre1": gain(ks_[11], (DEPTH, D_MODEL)),
        "w_qkv": nrm(ks_[12], (DEPTH, D_MODEL, QKV_WIDTH), D_MODEL ** -0.5),
        "lambda_q1": nrm(ks_[13], (DEPTH, HD_DIFF), 0.1),
        "lambda_k1": nrm(ks_[14], (DEPTH, HD_DIFF), 0.1),
        "lambda_q2": nrm(ks_[15], (DEPTH, HD_DIFF), 0.1),
        "lambda_k2": nrm(ks_[16], (DEPTH, HD_DIFF), 0.1),
        "g_subln": gain(ks_[17], (DEPTH, VD_DIFF)),
        "w_o": nrm(ks_[18], (DEPTH, MIX_WIDTH, D_MODEL), MIX_WIDTH ** -0.5),
        "g_post1": gain(ks_[19], (DEPTH, D_MODEL)),
        "g_pre2": gain(ks_[20], (DEPTH, D_MODEL)),
        "w_router_group": nrm(ks_[21], (DEPTH, D_MODEL, N_GROUPS), D_MODEL ** -0.5),
        "b_router_group": nrm(ks_[22], (DEPTH, N_GROUPS), 0.01),
        "w_router_expert": nrm(ks_[23], (DEPTH, D_MODEL, N_GROUPS, EXPERTS_PER_GROUP), D_MODEL ** -0.5),
        "b_router_expert": nrm(ks_[24], (DEPTH, N_GROUPS, EXPERTS_PER_GROUP), 0.01),
        "w_gate": nrm(ks_[25], (DEPTH, N_EXPERTS, D_MODEL, D_EXPERT), D_MODEL ** -0.5),
        "w_up": nrm(ks_[26], (DEPTH, N_EXPERTS, D_MODEL, D_EXPERT), D_MODEL ** -0.5),
        "w_down": nrm(ks_[27], (DEPTH, N_EXPERTS, D_EXPERT, D_MODEL), D_EXPERT ** -0.5),
        "g_post2": gain(ks_[28], (DEPTH, D_MODEL)),
    }


def reference(x_prompt, x_sample, c_prompt, c_sample, cache_diff_k, cache_diff_v, cache_sb_k, cache_sb_v,
              page_table, w_ada, b_ada, g_pre1, w_qkv, lambda_q1, lambda_k1, lambda_q2, lambda_k2, g_subln,
              w_o, g_post1, g_pre2, w_router_group, b_router_group, w_router_expert, b_router_expert,
              w_gate, w_up, w_down, g_post2):
    db = x_sample.shape[0]
    n_new = x_sample.shape[1]
    past = page_table.shape[1] * PAGE_SIZE
    pos_prompt = jnp.arange(x_prompt.shape[1])
    pos_sample = past + jnp.arange(n_new)
    xp, xs = x_prompt, x_sample
    p_dk, p_dv, p_sk, p_sv = [], [], [], []
    s_dk, s_dv, s_sk, s_sv = [], [], [], []
    for l in range(DEPTH):
        lam_init = 0.8 - 0.6 * math.exp(-0.3 * l)
        lp = dict(w_ada=w_ada[l], b_ada=b_ada[l], g_pre1=g_pre1[l], w_qkv=w_qkv[l],
                  lq1=lambda_q1[l], lk1=lambda_k1[l], lq2=lambda_q2[l], lk2=lambda_k2[l],
                  g_subln=g_subln[l], w_o=w_o[l], g_post1=g_post1[l], g_pre2=g_pre2[l],
                  w_rg=w_router_group[l], b_rg=b_router_group[l], w_re=w_router_expert[l],
                  b_re=b_router_expert[l], w_gate=w_gate[l], w_up=w_up[l], w_down=w_down[l],
                  g_post2=g_post2[l])
        xp, (kd, vd, ks, vs) = _layer(xp, c_prompt, pos_prompt, _attend_prompt, lp, lam_init)
        p_dk.append(kd); p_dv.append(vd); p_sk.append(ks); p_sv.append(vs)
        kd_p = cache_diff_k[l, page_table].reshape(db, past, H_DIFF, 2, HD_DIFF)
        vd_p = cache_diff_v[l, page_table].reshape(db, past, H_DIFF, VD_DIFF)
        ks_p = cache_sb_k[l, page_table].reshape(db, past, H_SB, HD_SB)
        vs_p = cache_sb_v[l, page_table].reshape(db, past, H_SB, HD_SB)

        def attend_s(qd, kd_n, vd_n, qs, ks_n, vs_n, lam, kd_p=kd_p, vd_p=vd_p, ks_p=ks_p, vs_p=vs_p):
            return _attend_sample(qd, kd_n, vd_n, qs, ks_n, vs_n, lam, kd_p, vd_p, ks_p, vs_p)

        xs, (kd, vd, ks, vs) = _layer(xs, c_sample, pos_sample, attend_s, lp, lam_init)
        s_dk.append(kd); s_dv.append(vd); s_sk.append(ks); s_sv.append(vs)
    return (xp, xs,
            jnp.stack(p_dk), jnp.stack(p_dv), jnp.stack(p_sk), jnp.stack(p_sv),
            jnp.stack(s_dk), jnp.stack(s_dv), jnp.stack(s_sk), jnp.stack(s_sv))
```

```python
import functools
import math

import jax
import jax.numpy as jnp
from jax import lax
from jax.experimental import pallas as pl
from jax.experimental.pallas import tpu as pltpu

F32 = jnp.float32
BF16 = jnp.bfloat16
I32 = jnp.int32

EPS = 1e-6
H_DIFF = 8
HD_DIFF = 64
VD_DIFF = 2 * HD_DIFF
W_DIFF = H_DIFF * VD_DIFF
H_SB = 16
HD_SB = 64
W_SB = H_SB * HD_SB
ROT_DIM = HD_DIFF // 4
ROPE_THETA = 500000.0
N_GROUPS = 4
EXPERTS_PER_GROUP = 8
N_EXPERTS = N_GROUPS * EXPERTS_PER_GROUP
ROUTER_LANES = 128
DIFF_SCALE = HD_DIFF ** -0.5
SB_SCALE = HD_SB ** -0.5
PAGE = 128
LANES = 128
MASK_VALUE = -1e30
VMEM_LIMIT = 56 * 1024 * 1024


def _cparams(sem):
    return pltpu.CompilerParams(dimension_semantics=sem, vmem_limit_bytes=VMEM_LIMIT)


def _split3(x):
    hi = x.astype(BF16)
    r = x - hi.astype(F32)
    mid = r.astype(BF16)
    lo = (r - mid.astype(F32)).astype(BF16)
    return hi, mid, lo


def _split(x):
    hi = x.astype(BF16)
    return hi, (x - hi.astype(F32)).astype(BF16)


def _dot(a, b):
    return jnp.dot(a, b, preferred_element_type=F32)


def _dot_nt(a, b):
    return lax.dot_general(a, b, (((1,), (1,)), ((), ())), preferred_element_type=F32)


def _dot3(a, b):
    ah, al = _split(a)
    bh, bl = _split(b)
    return _dot(ah, bh) + (_dot(ah, bl) + _dot(al, bh))


def _rms(x, g):
    return x * lax.rsqrt(jnp.mean(x * x, axis=-1, keepdims=True) + EPS) * g


def _silu(x):
    return x / (1.0 + jnp.exp(-x))


def _lam(lp, lam_init):
    a = jnp.sum(lp[0:1] * lp[1:2], axis=-1, keepdims=True)
    b = jnp.sum(lp[2:3] * lp[3:4], axis=-1, keepdims=True)
    return jnp.exp(a) - jnp.exp(b) + lam_init


def _log_sigmoid_pair(z):
    lsig = jnp.minimum(z, 0.0) - jnp.log(1.0 + jnp.exp(-jnp.abs(z)))
    return lsig, lsig - z


def _ada_kernel(c_ref, w_ref, b_ref, o_ref):
    o_ref[...] = _dot(_silu(c_ref[...]).astype(BF16), w_ref[...].astype(BF16)) + b_ref[...]


def _ada(c, w, b):
    r, d = c.shape
    n = w.shape[1]
    tn = min(n, 1024)
    return pl.pallas_call(
        _ada_kernel,
        grid=(n // tn,),
        in_specs=[pl.BlockSpec((r, d), lambda j: (0, 0)),
                  pl.BlockSpec((d, tn), lambda j: (0, j)),
                  pl.BlockSpec((1, tn), lambda j: (0, j))],
        out_specs=pl.BlockSpec((r, tn), lambda j: (0, j)),
        out_shape=jax.ShapeDtypeStruct((r, n), F32),
        compiler_params=_cparams(("arbitrary",)),
        name="ada",
    )(c, w, b.reshape(1, n))


def _qkv_kernel(x_ref, sh_ref, sc_ref, g_ref, w_ref, cos_ref, sa_ref, sb_ref, *rest, out_map):
    outs, h_scr = rest[:-1], rest[-1]
    j = pl.program_id(1)

    @pl.when(j == 0)
    def _():
        h = _rms(x_ref[...], g_ref[...]) * (1.0 + sc_ref[...]) + sh_ref[...]
        h_scr[...] = h.astype(h_scr.dtype)

    acc = _dot(h_scr[...], w_ref[...])
    width = acc.shape[1]

    def store(idxs, fn):
        for c in range(width // LANES):
            val = fn(acc[:, c * LANES:(c + 1) * LANES])
            for i in idxs:
                outs[i][:, c * LANES:(c + 1) * LANES] = val.astype(outs[i].dtype)

    def rope(scale):
        def fn(y):
            r = y * cos_ref[...] + pltpu.roll(y, LANES - ROT_DIM // 2, 1) * sa_ref[...] \
                + pltpu.roll(y, ROT_DIM // 2, 1) * sb_ref[...]
            return r if scale is None else r * scale
        return fn

    fns = [rope(DIFF_SCALE), rope(None), lambda y: y, lambda y: y * SB_SCALE, lambda y: y, lambda y: y]
    for jj in range(6):
        pl.when(j == jj)(functools.partial(store, out_map[jj], fns[jj]))


def _qkv(x, sh, sc, g, w, cos, sa, sb, *, tm, rows_per_mod, rope_rows, with_bf16):
    t, d = x.shape
    wc = W_DIFF
    nt = t // tm
    if rows_per_mod is None:
        mod_spec = pl.BlockSpec((tm, d), lambda i, j: (i, 0))
    else:
        per = rows_per_mod // tm
        mod_spec = pl.BlockSpec((None, 1, d), lambda i, j: (i // per, 0, 0))
    rt = rope_rows // tm
    rope_spec = pl.BlockSpec((tm, LANES), lambda i, j: (i % rt, 0))
    if with_bf16:
        out_dtypes = [BF16, F32, BF16, F32, BF16, BF16, F32, BF16, F32, BF16]
        out_map = [[0], [1, 2], [3, 4], [5], [6, 7], [8, 9]]
    else:
        out_dtypes = [F32] * 6
        out_map = [[0], [1], [2], [3], [4], [5]]
    o_spec = pl.BlockSpec((tm, wc), lambda i, j: (i, 0))
    return pl.pallas_call(
        functools.partial(_qkv_kernel, out_map=out_map),
        grid=(nt, 6),
        in_specs=[pl.BlockSpec((tm, d), lambda i, j: (i, 0)), mod_spec, mod_spec,
                  pl.BlockSpec((1, d), lambda i, j: (0, 0)),
                  pl.BlockSpec((d, wc), lambda i, j: (0, j)),
                  rope_spec, rope_spec, rope_spec],
        out_specs=[o_spec] * len(out_dtypes),
        out_shape=[jax.ShapeDtypeStruct((t, wc), dt) for dt in out_dtypes],
        scratch_shapes=[pltpu.VMEM((tm, d), BF16)],
        compiler_params=_cparams(("arbitrary", "arbitrary")),
        name="qkv" if with_bf16 else "qkv_sample",
    )(x, sh, sc, g.reshape(1, d), w, cos, sa, sb)


def _rope_tables(pos):
    half = ROT_DIM // 2
    inv_freq = jnp.power(jnp.float32(ROPE_THETA), -jnp.arange(half, dtype=F32) * (2.0 / ROT_DIM))
    lane = jnp.arange(LANES)
    l64 = lane % HD_DIFF
    ang = pos.astype(F32)[:, None] * inv_freq[l64 % half][None, :]
    rot = (l64 < ROT_DIM)[None, :]
    first = (l64 < half)[None, :]
    cos = jnp.where(rot, jnp.cos(ang), 1.0)
    sa = jnp.where(first, -jnp.sin(ang), 0.0)
    sb = jnp.where(rot & ~first, jnp.sin(ang), 0.0)
    return cos, sa, sb


def _diff_attn_kernel(lam_ref, g_ref, q_ref, k_ref, v_ref, o_ref, *, tq, lam_init):
    qi = pl.program_id(2)
    lam = _lam(lam_ref[...], lam_init)
    q = q_ref[...]
    lane = lax.broadcasted_iota(I32, (tq, LANES), 1)
    zero = jnp.zeros_like(q)
    qq = jnp.concatenate([jnp.where(lane < HD_DIFF, q, zero), jnp.where(lane >= HD_DIFF, q, zero)], axis=0)

    def step(kb, carry, masked):
        m, l, acc = carry
        off = pl.multiple_of(kb * tq, tq)
        k = k_ref[pl.ds(off, tq), :]
        v = v_ref[pl.ds(off, tq), :]
        s = _dot_nt(qq, k)
        if masked:
            r = lax.broadcasted_iota(I32, (2 * tq, tq), 0)
            r = jnp.where(r >= tq, r - tq, r)
            c = lax.broadcasted_iota(I32, (2 * tq, tq), 1)
            s = jnp.where(c <= r, s, MASK_VALUE)
        m_new = jnp.maximum(m, jnp.max(s, axis=1, keepdims=True))
        alpha = jnp.exp(m - m_new)
        p = jnp.exp(s - m_new)
        l = alpha * l + jnp.sum(p, axis=1, keepdims=True)
        acc = alpha * acc + _dot(p.astype(BF16), v)
        return m_new, l, acc

    init = (jnp.full((2 * tq, 1), MASK_VALUE, F32), jnp.zeros((2 * tq, 1), F32),
            jnp.zeros((2 * tq, LANES), F32))
    carry = lax.fori_loop(0, qi, lambda kb, c: step(kb, c, False), init)
    _, l, acc = step(qi, carry, True)
    o = acc / l
    od = o[:tq] - lam * o[tq:]
    o_ref[...] = (_rms(od, g_ref[...]) * (1.0 - lam_init)).astype(o_ref.dtype)


def _diff_attn(lam_p, g_subln, q, k, v, *, b, s, tq, lam_init):
    nq = s // tq
    return pl.pallas_call(
        functools.partial(_diff_attn_kernel, tq=tq, lam_init=lam_init),
        grid=(b, H_DIFF, nq),
        in_specs=[pl.BlockSpec((4, HD_DIFF), lambda bi, h, qi: (0, 0)),
                  pl.BlockSpec((1, VD_DIFF), lambda bi, h, qi: (0, 0)),
                  pl.BlockSpec((tq, LANES), lambda bi, h, qi: (bi * nq + qi, h)),
                  pl.BlockSpec((s, LANES), lambda bi, h, qi: (bi, h)),
                  pl.BlockSpec((s, LANES), lambda bi, h, qi: (bi, h))],
        out_specs=pl.BlockSpec((tq, LANES), lambda bi, h, qi: (bi * nq + qi, h)),
        out_shape=jax.ShapeDtypeStruct((b * s, W_DIFF), BF16),
        compiler_params=_cparams(("arbitrary", "arbitrary", "arbitrary")),
        name="diff_attn",
    )(lam_p, g_subln.reshape(1, VD_DIFF), q, k, v)


def _sb_attn_kernel(q_ref, k_ref, v_ref, o_ref, *, tq):
    qi = pl.program_id(2)
    q = q_ref[...]
    lane = lax.broadcasted_iota(I32, (tq, LANES), 1)
    r = lax.broadcasted_iota(I32, (tq, tq), 0)
    c = lax.broadcasted_iota(I32, (tq, tq), 1)
    strict = c < r
    later = (r > c).astype(BF16)
    zero = jnp.zeros_like(q)
    accs = []
    for hh in range(2):
        qh = jnp.where((lane >= hh * HD_SB) & (lane < (hh + 1) * HD_SB), q, zero)

        def step(kb, carry, masked, qh=qh):
            run, acc = carry
            off = pl.multiple_of(kb * tq, tq)
            k = k_ref[pl.ds(off, tq), :]
            v = v_ref[pl.ds(off, tq), :]
            z = _dot_nt(qh, k)
            lsig, lom = _log_sigmoid_pair(z)
            if masked:
                lom = jnp.where(strict, lom, 0.0)
            hi, lo = _split(lom)
            after = _dot(hi, later) + _dot(lo, later)
            w = jnp.exp(lsig + after + run)
            if masked:
                w = jnp.where(strict, w, 0.0)
            acc = acc + _dot(w.astype(BF16), v)
            run = run + jnp.sum(lom, axis=1, keepdims=True)
            return run, acc

        carry = step(qi, (jnp.zeros((tq, 1), F32), jnp.zeros((tq, LANES), F32)), True)
        _, acc = lax.fori_loop(0, qi, lambda t, cr, step=step: step(qi - 1 - t, cr, False), carry)
        accs.append(acc)
    o_ref[...] = jnp.where(lane < HD_SB, accs[0], accs[1]).astype(o_ref.dtype)


def _sb_attn(q, k, v, *, b, s, tq):
    nq = s // tq
    return pl.pallas_call(
        functools.partial(_sb_attn_kernel, tq=tq),
        grid=(b, H_SB // 2, nq),
        in_specs=[pl.BlockSpec((tq, LANES), lambda bi, h, qi: (bi * nq + qi, h)),
                  pl.BlockSpec((s, LANES), lambda bi, h, qi: (bi, h)),
                  pl.BlockSpec((s, LANES), lambda bi, h, qi: (bi, h))],
        out_specs=pl.BlockSpec((tq, LANES), lambda bi, h, qi: (bi * nq + qi, h)),
        out_shape=jax.ShapeDtypeStruct((b * s, W_SB), BF16),
        compiler_params=_cparams(("arbitrary", "arbitrary", "arbitrary")),
        name="sb_attn",
    )(q, k, v)


def _sample_attn_kernel(pt_ref, lam_ref, g_ref, qd_ref, qs_ref, kn_ref, vn_ref,
                        ckd_ref, cvd_ref, cks_ref, cvs_ref, o_ref,
                        qdr_scr, qsr_scr, m_scr, l_scr, accd_scr, run_scr, accs_scr, *, n_pages, lam_init):
    del pt_ref
    p = pl.program_id(1)
    nr = 2 * H_DIFF
    w = W_DIFF

    @pl.when(p == 0)
    def _():
        qd = qd_ref[...]
        qdr_scr[...] = jnp.transpose(jnp.broadcast_to(qd, (PAGE, w)))
        qsr_scr[...] = jnp.transpose(jnp.broadcast_to(qs_ref[...], (PAGE, w)))
        row = lax.broadcasted_iota(I32, (nr, w), 0)
        lane = lax.broadcasted_iota(I32, (nr, w), 1)
        s_self = jnp.sum(jnp.where(jnp.right_shift(lane, 6) == row, qd * kn_ref[...], 0.0), axis=1, keepdims=True)
        m_scr[...] = s_self
        l_scr[...] = jnp.ones_like(l_scr)
        accd_scr[...] = jnp.broadcast_to(vn_ref[...], accd_scr.shape)
        run_scr[...] = jnp.zeros_like(run_scr)
        accs_scr[...] = jnp.zeros_like(accs_scr)

    sd = jnp.sum((ckd_ref[...] * qdr_scr[...]).reshape(nr, HD_DIFF, PAGE), axis=1)
    m_old = m_scr[...]
    m_new = jnp.maximum(m_old, jnp.max(sd, axis=1, keepdims=True))
    alpha = jnp.exp(m_old - m_new)
    pw = jnp.exp(sd - m_new)
    m_scr[...] = m_new
    l_scr[...] = alpha * l_scr[...] + jnp.sum(pw, axis=1, keepdims=True)
    accd_scr[...] = alpha * accd_scr[...] + _dot3(pw, cvd_ref[...])

    z = jnp.sum((cks_ref[...] * qsr_scr[...]).reshape(H_SB, HD_SB, PAGE), axis=1)
    lsig, lom = _log_sigmoid_pair(z)
    t = lax.broadcasted_iota(I32, (PAGE, PAGE), 0)
    u = lax.broadcasted_iota(I32, (PAGE, PAGE), 1)
    later = (t > u).astype(BF16)
    hi, mid, lo = _split3(lom)
    after = _dot(hi, later) + (_dot(mid, later) + _dot(lo, later))
    wgt = jnp.exp(lsig + after + run_scr[...])
    wrep = jnp.broadcast_to(wgt[:, None, :], (H_SB, HD_SB, PAGE)).reshape(w, PAGE)
    accs_scr[...] = accs_scr[...] + wrep * cvs_ref[...]
    run_scr[...] = run_scr[...] + jnp.sum(lom, axis=1, keepdims=True)

    @pl.when(p == n_pages - 1)
    def _():
        lam = _lam(lam_ref[...], lam_init)
        row = lax.broadcasted_iota(I32, (nr, w), 0)
        lane = lax.broadcasted_iota(I32, (nr, w), 1)
        o = accd_scr[...] / l_scr[...]
        o = jnp.where(jnp.bitwise_and(row, 1) == 0, o, -lam * o)
        o = jnp.where(jnp.right_shift(lane, 7) == jnp.right_shift(row, 1), o, 0.0)
        od = jnp.sum(o, axis=0, keepdims=True)
        row8 = lax.broadcasted_iota(I32, (H_DIFF, w), 0)
        head8 = jnp.right_shift(lax.broadcasted_iota(I32, (H_DIFF, w), 1), 7) == row8
        ss = jnp.sum(jnp.where(head8, od * od, 0.0), axis=1, keepdims=True) * (1.0 / VD_DIFF)
        inv = jnp.sum(jnp.where(head8, lax.rsqrt(ss + EPS), 0.0), axis=0, keepdims=True)
        o_ref[:, :w] = od * inv * g_ref[...] * (1.0 - lam_init)
        hi, mid, lo = _split3(accs_scr[...])
        ones = jnp.ones((8, PAGE), BF16)
        osb = _dot_nt(ones, hi) + (_dot_nt(ones, mid) + _dot_nt(ones, lo))
        o_ref[:, w:] = osb[0:1]


def _sample_attn(page_table, lam_p, g_subln, qd, qs, kn, vn, ckd, cvd, cks, cvs, *, lam_init):
    db, n_pages = page_table.shape
    w = W_DIFF
    row3 = lambda a: a.reshape(db, 1, a.shape[-1])
    row_spec = pl.BlockSpec((None, 1, w), lambda b, p, pt: (b, 0, 0))
    fm_spec = pl.BlockSpec((None, w, PAGE), lambda b, p, pt: (pt[b, n_pages - 1 - p], 0, 0))
    tm_spec = pl.BlockSpec((None, PAGE, w), lambda b, p, pt: (pt[b, n_pages - 1 - p], 0, 0))
    nr = 2 * H_DIFF
    out = pl.pallas_call(
        functools.partial(_sample_attn_kernel, n_pages=n_pages, lam_init=lam_init),
        grid_spec=pltpu.PrefetchScalarGridSpec(
            num_scalar_prefetch=1,
            grid=(db, n_pages),
            in_specs=[pl.BlockSpec((4, HD_DIFF), lambda b, p, pt: (0, 0)),
                      pl.BlockSpec((1, w), lambda b, p, pt: (0, 0)),
                      row_spec, row_spec, row_spec, row_spec,
                      fm_spec, tm_spec, fm_spec, fm_spec],
            out_specs=pl.BlockSpec((None, 1, 2 * w), lambda b, p, pt: (b, 0, 0)),
            scratch_shapes=[pltpu.VMEM((w, PAGE), F32), pltpu.VMEM((w, PAGE), F32),
                            pltpu.VMEM((nr, 1), F32), pltpu.VMEM((nr, 1), F32), pltpu.VMEM((nr, w), F32),
                            pltpu.VMEM((H_SB, 1), F32), pltpu.VMEM((w, PAGE), F32)]),
        out_shape=jax.ShapeDtypeStruct((db, 1, 2 * w), F32),
        compiler_params=_cparams(("arbitrary", "arbitrary")),
        name="sample_attn",
    )(page_table, lam_p, jnp.tile(g_subln, H_DIFF).reshape(1, w), row3(qd), row3(qs), row3(kn), row3(vn),
      ckd, cvd, cks, cvs)
    return out.reshape(db, 2 * w)


def _route(lg):
    lane = lax.broadcasted_iota(I32, lg.shape, 1)
    lanef = lane.astype(F32)
    big = float(ROUTER_LANES)
    gmask = lane < N_GROUPS
    gmax = jnp.max(jnp.where(gmask, lg, MASK_VALUE), axis=1, keepdims=True)
    gidx = jnp.min(jnp.where(gmask & (lg == gmax), lanef, big), axis=1, keepdims=True).astype(I32)
    gsum = jnp.sum(jnp.where(gmask, jnp.exp(lg - gmax), 0.0), axis=1, keepdims=True)
    gw = 1.0 / gsum
    emask = (lane >= N_GROUPS) & (lane < N_GROUPS + N_EXPERTS) & (jnp.right_shift(lane - N_GROUPS, 3) == gidx)
    v1 = jnp.max(jnp.where(emask, lg, MASK_VALUE), axis=1, keepdims=True)
    i1 = jnp.min(jnp.where(emask & (lg == v1), lanef, big), axis=1, keepdims=True)
    emask2 = emask & (lanef != i1)
    v2 = jnp.max(jnp.where(emask2, lg, MASK_VALUE), axis=1, keepdims=True)
    i2 = jnp.min(jnp.where(emask2 & (lg == v2), lanef, big), axis=1, keepdims=True)
    e2 = jnp.exp(v2 - v1)
    t1 = 1.0 / (1.0 + e2)
    t2 = e2 / (1.0 + e2)
    return gw * (jnp.where(lanef == i1, t1, 0.0) + jnp.where(lanef == i2, t2, 0.0))


def _oproj_kernel(*refs, n_mix, nk):
    mix_refs = refs[:n_mix]
    (w_ref, x_ref, gpost_ref, g1_ref, gpre2_ref, sc2_ref, sh2_ref, wr_ref, br_ref,
     x1_ref, h2_ref, comb_ref, acc_scr) = refs[n_mix:]
    k = pl.program_id(1)

    @pl.when(k == 0)
    def _():
        acc_scr[...] = jnp.zeros_like(acc_scr)

    if n_mix == 1:
        acc_scr[...] += _dot(mix_refs[0][...].astype(BF16), w_ref[...])
    else:
        for idx in range(n_mix):
            @pl.when(k == idx)
            def _(idx=idx):
                acc_scr[...] += _dot(mix_refs[idx][...].astype(BF16), w_ref[...])

    @pl.when(k == nk - 1)
    def _():
        x1 = x_ref[...] + g1_ref[...] * _rms(acc_scr[...], gpost_ref[...])
        h2 = (_rms(x1, gpre2_ref[...]) * (1.0 + sc2_ref[...]) + sh2_ref[...]).astype(BF16)
        x1_ref[...] = x1
        h2_ref[...] = h2
        comb_ref[...] = _route(_dot(h2, wr_ref[...]) + br_ref[...])


def _oproj(mixes, w, x, gpost, g1, gpre2, sc2, sh2, wr, br, *, tm, tk, rows_per_mod):
    t, d = x.shape
    n_mix = len(mixes)
    nk = w.shape[0] // tk
    if rows_per_mod is None:
        mod_spec = pl.BlockSpec((tm, d), lambda i, k: (i, 0))
    else:
        per = rows_per_mod // tm
        mod_spec = pl.BlockSpec((None, 1, d), lambda i, k: (i // per, 0, 0))
    if n_mix == 1:
        mix_specs = [pl.BlockSpec((tm, tk), lambda i, k: (i, k))]
    else:
        mix_specs = [pl.BlockSpec((tm, tk), lambda i, k: (i, 0))] * n_mix
    vec = pl.BlockSpec((1, d), lambda i, k: (0, 0))
    row = pl.BlockSpec((tm, d), lambda i, k: (i, 0))
    return pl.pallas_call(
        functools.partial(_oproj_kernel, n_mix=n_mix, nk=nk),
        grid=(t // tm, nk),
        in_specs=mix_specs + [pl.BlockSpec((tk, d), lambda i, k: (k, 0)), row, vec, mod_spec, vec,
                              mod_spec, mod_spec,
                              pl.BlockSpec((d, ROUTER_LANES), lambda i, k: (0, 0)),
                              pl.BlockSpec((1, ROUTER_LANES), lambda i, k: (0, 0))],
        out_specs=[row, row, pl.BlockSpec((tm, ROUTER_LANES), lambda i, k: (i, 0))],
        out_shape=[jax.ShapeDtypeStruct((t, d), F32), jax.ShapeDtypeStruct((t, d), BF16),
                   jax.ShapeDtypeStruct((t, ROUTER_LANES), F32)],
        scratch_shapes=[pltpu.VMEM((tm, d), F32)],
        compiler_params=_cparams(("arbitrary", "arbitrary")),
        name="oproj",
    )(*mixes, w, x, gpost.reshape(1, d), g1, gpre2.reshape(1, d), sc2, sh2, wr, br)


def _moe_kernel(h_ref, comb_ref, wg_ref, wu_ref, wd_ref, x1_ref, g2_ref, gpost_ref, y_ref, acc_scr, *, n_e):
    e = pl.program_id(1)

    @pl.when(e == 0)
    def _():
        acc_scr[...] = jnp.zeros_like(acc_scr)

    h = h_ref[...]
    act = _silu(_dot(h, wg_ref[...])) * _dot(h, wu_ref[...])
    comb = comb_ref[...]
    lane = lax.broadcasted_iota(I32, comb.shape, 1)
    ce = jnp.sum(jnp.where(lane == e + N_GROUPS, comb, 0.0), axis=1, keepdims=True)
    acc_scr[...] += ce * _dot(act.astype(BF16), wd_ref[...])

    @pl.when(e == n_e - 1)
    def _():
        y_ref[...] = x1_ref[...] + g2_ref[...] * _rms(acc_scr[...], gpost_ref[...])


def _moe(h2, comb, wg, wu, wd, x1, g2, gpost, *, tm, rows_per_mod):
    t, d = x1.shape
    n_e, _, de = wg.shape
    if rows_per_mod is None:
        mod_spec = pl.BlockSpec((tm, d), lambda i, e: (i, 0))
    else:
        per = rows_per_mod // tm
        mod_spec = pl.BlockSpec((None, 1, d), lambda i, e: (i // per, 0, 0))
    row = pl.BlockSpec((tm, d), lambda i, e: (i, 0))
    return pl.pallas_call(
        functools.partial(_moe_kernel, n_e=n_e),
        grid=(t // tm, n_e),
        in_specs=[row, pl.BlockSpec((tm, ROUTER_LANES), lambda i, e: (i, 0)),
                  pl.BlockSpec((None, d, de), lambda i, e: (e, 0, 0)),
                  pl.BlockSpec((None, d, de), lambda i, e: (e, 0, 0)),
                  pl.BlockSpec((None, de, d), lambda i, e: (e, 0, 0)),
                  row, mod_spec, pl.BlockSpec((1, d), lambda i, e: (0, 0))],
        out_specs=row,
        out_shape=jax.ShapeDtypeStruct((t, d), F32),
        scratch_shapes=[pltpu.VMEM((tm, d), F32)],
        compiler_params=_cparams(("arbitrary", "arbitrary")),
        name="moe",
    )(h2, comb, wg, wu, wd, x1, g2, gpost.reshape(1, d))


def _tile(n, want):
    return want if n % want == 0 else n


def _feature_major(cache):
    nd = cache.ndim
    return jnp.transpose(cache, (0,) + tuple(range(2, nd)) + (1,)).reshape(cache.shape[0], W_DIFF, PAGE)


def kernel(x_prompt, x_sample, c_prompt, c_sample, cache_diff_k, cache_diff_v, cache_sb_k, cache_sb_v, page_table, w_ada, b_ada, g_pre1, w_qkv, lambda_q1, lambda_k1, lambda_q2, lambda_k2, g_subln, w_o, g_post1, g_pre2, w_router_group, b_router_group, w_router_expert, b_router_expert, w_gate, w_up, w_down, g_post2):
    b, s, d = x_prompt.shape
    db = x_sample.shape[0]
    depth = w_ada.shape[0]
    n_pages = page_table.shape[1]
    n_pool = cache_diff_k.shape[1]
    t = b * s
    xp = x_prompt.reshape(t, d)
    xs = x_sample.reshape(db, d)
    rows = b + db
    rows_pad = -(-rows // 8) * 8
    c_all = jnp.concatenate([c_prompt, c_sample, jnp.zeros((rows_pad - rows, d), F32)], axis=0)
    rope_p = _rope_tables(jnp.arange(s))
    rope_s = _rope_tables(jnp.full((db,), n_pages * PAGE))

    tm_qkv = _tile(s, 512)
    tq_diff = _tile(s, 256)
    tq_sb = _tile(s, 128)
    tm_o = _tile(s, 256)
    tm_moe = _tile(s, 512)

    outs = [[] for _ in range(8)]
    for l in range(depth):
        lam_init = 0.8 - 0.6 * math.exp(-0.3 * l)
        lam_p = jnp.stack([lambda_q1[l], lambda_k1[l], lambda_q2[l], lambda_k2[l]])
        mod = _ada(c_all, w_ada[l], b_ada[l])
        mp = [mod[:b, i * d:(i + 1) * d].reshape(b, 1, d) for i in range(6)]
        ms = [mod[b:b + db, i * d:(i + 1) * d] for i in range(6)]
        pad = ROUTER_LANES - N_GROUPS - N_EXPERTS
        wr = jnp.concatenate([w_router_group[l], w_router_expert[l].reshape(d, N_EXPERTS),
                              jnp.zeros((d, pad), F32)], axis=1).astype(BF16)
        br = jnp.concatenate([b_router_group[l], b_router_expert[l].reshape(N_EXPERTS),
                              jnp.zeros((pad,), F32)]).reshape(1, ROUTER_LANES)
        wqkv16, wo16 = w_qkv[l].astype(BF16), w_o[l].astype(BF16)
        wg16, wu16, wd16 = w_gate[l].astype(BF16), w_up[l].astype(BF16), w_down[l].astype(BF16)

        qd, kd, kd16, vd, vd16, qs, ks, ks16, vs, vs16 = _qkv(
            xp, mp[0], mp[1], g_pre1[l], wqkv16, *rope_p, tm=tm_qkv, rows_per_mod=s, rope_rows=s, with_bf16=True)
        mix_d = _diff_attn(lam_p, g_subln[l], qd, kd16, vd16, b=b, s=s, tq=tq_diff, lam_init=lam_init)
        mix_s = _sb_attn(qs, ks16, vs16, b=b, s=s, tq=tq_sb)
        x1, h2, comb = _oproj([mix_d, mix_s], wo16, xp, g_post1[l], mp[2], g_pre2[l], mp[4], mp[3],
                              wr, br, tm=tm_o, tk=W_DIFF, rows_per_mod=s)
        xp = _moe(h2, comb, wg16, wu16, wd16, x1, mp[5], g_post2[l], tm=tm_moe, rows_per_mod=s)
        for lst, a in zip(outs[:4], (kd, vd, ks, vs)):
            lst.append(a)

        qd, kd, vd, qs, ks, vs = _qkv(xs, ms[0], ms[1], g_pre1[l], wqkv16, *rope_s,
                                      tm=db, rows_per_mod=None, rope_rows=db, with_bf16=False)
        mix = _sample_attn(page_table, lam_p, g_subln[l], qd, qs, kd, vd,
                           _feature_major(cache_diff_k[l]), cache_diff_v[l].reshape(n_pool, PAGE, W_DIFF),
                           _feature_major(cache_sb_k[l]), _feature_major(cache_sb_v[l]), lam_init=lam_init)
        x1, h2, comb = _oproj([mix], wo16, xs, g_post1[l], ms[2], g_pre2[l], ms[4], ms[3], wr, br,
                              tm=db, tk=W_DIFF, rows_per_mod=None)
        xs = _moe(h2, comb, wg16, wu16, wd16, x1, ms[5], g_post2[l], tm=db, rows_per_mod=None)
        for lst, a in zip(outs[4:], (kd, vd, ks, vs)):
            lst.append(a)

    def stack(lst, lead, tail):
        return jnp.stack(lst).reshape((depth,) + lead + tail)

    return (xp.reshape(b, s, d), xs.reshape(db, 1, d),
            stack(outs[0], (b, s), (H_DIFF, 2, HD_DIFF)), stack(outs[1], (b, s), (H_DIFF, VD_DIFF)),
            stack(outs[2], (b, s), (H_SB, HD_SB)), stack(outs[3], (b, s), (H_SB, HD_SB)),
            stack(outs[4], (db, 1), (H_DIFF, 2, HD_DIFF)), stack(outs[5], (db, 1), (H_DIFF, VD_DIFF)),
            stack(outs[6], (db, 1), (H_SB, HD_SB)), stack(outs[7], (db, 1), (H_SB, HD_SB)))
```

```python
import functools
import math

import jax
import jax.numpy as jnp
from jax import lax
from jax.experimental import pallas as pl
from jax.experimental.pallas import tpu as pltpu

F32 = jnp.float32
BF16 = jnp.bfloat16
I32 = jnp.int32

EPS = 1e-6
H_DIFF = 8
HD_DIFF = 64
VD_DIFF = 2 * HD_DIFF
W_DIFF = H_DIFF * VD_DIFF
H_SB = 16
HD_SB = 64
W_SB = H_SB * HD_SB
ROT_DIM = HD_DIFF // 4
ROPE_THETA = 500000.0
N_GROUPS = 4
EXPERTS_PER_GROUP = 8
N_EXPERTS = N_GROUPS * EXPERTS_PER_GROUP
ROUTER_LANES = 128
DIFF_SCALE = HD_DIFF ** -0.5
SB_SCALE = HD_SB ** -0.5
PAGE = 128
LANES = 128
MASK_VALUE = -1e30
EXP_UNDERFLOW = -104.0
VMEM_LIMIT = 56 * 1024 * 1024


def _cparams(sem):
    return pltpu.CompilerParams(dimension_semantics=sem, vmem_limit_bytes=VMEM_LIMIT)


def _split3(x):
    hi = x.astype(BF16)
    r = x - hi.astype(F32)
    mid = r.astype(BF16)
    lo = (r - mid.astype(F32)).astype(BF16)
    return hi, mid, lo


def _split(x):
    hi = x.astype(BF16)
    return hi, (x - hi.astype(F32)).astype(BF16)


def _dot(a, b):
    return jnp.dot(a, b, preferred_element_type=F32)


def _dot_nt(a, b):
    return lax.dot_general(a, b, (((1,), (1,)), ((), ())), preferred_element_type=F32)


def _dot3(a, b):
    ah, al = _split(a)
    bh, bl = _split(b)
    return _dot(ah, bh) + (_dot(ah, bl) + _dot(al, bh))


def _rms(x, g):
    return x * lax.rsqrt(jnp.mean(x * x, axis=-1, keepdims=True) + EPS) * g


def _silu(x):
    return x / (1.0 + jnp.exp(-x))


def _lam(lp, lam_init):
    a = jnp.sum(lp[0:1] * lp[1:2], axis=-1, keepdims=True)
    b = jnp.sum(lp[2:3] * lp[3:4], axis=-1, keepdims=True)
    return jnp.exp(a) - jnp.exp(b) + lam_init


def _log_sigmoid_pair(z):
    lsig = jnp.minimum(z, 0.0) - jnp.log(1.0 + jnp.exp(-jnp.abs(z)))
    return lsig, lsig - z


def _ada_kernel(c_ref, w_ref, b_ref, o_ref):
    o_ref[...] = _dot3(_silu(c_ref[...]), w_ref[...]) + b_ref[...]


def _ada(c, w, b):
    r, d = c.shape
    n = w.shape[1]
    tn = min(n, 512)
    return pl.pallas_call(
        _ada_kernel,
        grid=(n // tn,),
        in_specs=[pl.BlockSpec((r, d), lambda j: (0, 0)),
                  pl.BlockSpec((d, tn), lambda j: (0, j)),
                  pl.BlockSpec((1, tn), lambda j: (0, j))],
        out_specs=pl.BlockSpec((r, tn), lambda j: (0, j)),
        out_shape=jax.ShapeDtypeStruct((r, n), F32),
        compiler_params=_cparams(("arbitrary",)),
        name="ada",
    )(c, w, b.reshape(1, n))


def _qkv_kernel(x_ref, sh_ref, sc_ref, g_ref, w_ref, cos_ref, sa_ref, sb_ref, *rest, out_map):
    outs, h_scr = rest[:-1], rest[-1]
    j = pl.program_id(1)

    @pl.when(j == 0)
    def _():
        h = _rms(x_ref[...], g_ref[...]) * (1.0 + sc_ref[...]) + sh_ref[...]
        h_scr[...] = h.astype(h_scr.dtype)

    acc = (_dot3 if w_ref.dtype == F32 else _dot)(h_scr[...], w_ref[...])
    width = acc.shape[1]

    def store(idxs, fn):
        for c in range(width // LANES):
            val = fn(acc[:, c * LANES:(c + 1) * LANES])
            for i in idxs:
                outs[i][:, c * LANES:(c + 1) * LANES] = val.astype(outs[i].dtype)

    def rope(scale):
        def fn(y):
            r = y * cos_ref[...] + pltpu.roll(y, LANES - ROT_DIM // 2, 1) * sa_ref[...] \
                + pltpu.roll(y, ROT_DIM // 2, 1) * sb_ref[...]
            return r if scale is None else r * scale
        return fn

    fns = [rope(DIFF_SCALE), rope(None), lambda y: y, lambda y: y * SB_SCALE, lambda y: y, lambda y: y]
    for jj in range(6):
        pl.when(j == jj)(functools.partial(store, out_map[jj], fns[jj]))


def _qkv(x, sh, sc, g, w, cos, sa, sb, *, tm, rows_per_mod, rope_rows, with_bf16):
    t, d = x.shape
    wc = W_DIFF
    nt = t // tm
    if rows_per_mod is None:
        mod_spec = pl.BlockSpec((tm, d), lambda i, j: (i, 0))
    else:
        per = rows_per_mod // tm
        mod_spec = pl.BlockSpec((None, 1, d), lambda i, j: (i // per, 0, 0))
    rt = rope_rows // tm
    rope_spec = pl.BlockSpec((tm, LANES), lambda i, j: (i % rt, 0))
    if with_bf16:
        out_dtypes = [BF16, F32, BF16, F32, BF16, BF16, F32, BF16, F32, BF16]
        out_map = [[0], [1, 2], [3, 4], [5], [6, 7], [8, 9]]
    else:
        out_dtypes = [F32] * 6
        out_map = [[0], [1], [2], [3], [4], [5]]
    o_spec = pl.BlockSpec((tm, wc), lambda i, j: (i, 0))
    return pl.pallas_call(
        functools.partial(_qkv_kernel, out_map=out_map),
        grid=(nt, 6),
        in_specs=[pl.BlockSpec((tm, d), lambda i, j: (i, 0)), mod_spec, mod_spec,
                  pl.BlockSpec((1, d), lambda i, j: (0, 0)),
                  pl.BlockSpec((d, wc), lambda i, j: (0, j)),
                  rope_spec, rope_spec, rope_spec],
        out_specs=[o_spec] * len(out_dtypes),
        out_shape=[jax.ShapeDtypeStruct((t, wc), dt) for dt in out_dtypes],
        scratch_shapes=[pltpu.VMEM((tm, d), w.dtype)],
        compiler_params=_cparams(("arbitrary", "arbitrary")),
        name="qkv" if with_bf16 else "qkv_sample",
    )(x, sh, sc, g.reshape(1, d), w, cos, sa, sb)


def _rope_tables(pos):
    half = ROT_DIM // 2
    inv_freq = jnp.power(jnp.float32(ROPE_THETA), -jnp.arange(half, dtype=F32) * (2.0 / ROT_DIM))
    lane = jnp.arange(LANES)
    l64 = lane % HD_DIFF
    ang = pos.astype(F32)[:, None] * inv_freq[l64 % half][None, :]
    rot = (l64 < ROT_DIM)[None, :]
    first = (l64 < half)[None, :]
    cos = jnp.where(rot, jnp.cos(ang), 1.0)
    sa = jnp.where(first, -jnp.sin(ang), 0.0)
    sb = jnp.where(rot & ~first, jnp.sin(ang), 0.0)
    return cos, sa, sb


def _diff_attn_kernel(lam_ref, g_ref, q_ref, k_ref, v_ref, o_ref, *, tq, lam_init):
    qi = pl.program_id(2)
    lam = _lam(lam_ref[...], lam_init)
    q = q_ref[...]
    lane = lax.broadcasted_iota(I32, (tq, LANES), 1)
    zero = jnp.zeros_like(q)
    qq = jnp.concatenate([jnp.where(lane < HD_DIFF, q, zero), jnp.where(lane >= HD_DIFF, q, zero)], axis=0)

    def step(kb, carry, masked):
        m, l, acc = carry
        off = pl.multiple_of(kb * tq, tq)
        k = k_ref[pl.ds(off, tq), :]
        v = v_ref[pl.ds(off, tq), :]
        s = _dot_nt(qq, k)
        if masked:
            r = lax.broadcasted_iota(I32, (2 * tq, tq), 0)
            r = jnp.where(r >= tq, r - tq, r)
            c = lax.broadcasted_iota(I32, (2 * tq, tq), 1)
            s = jnp.where(c <= r, s, MASK_VALUE)
        m_new = jnp.maximum(m, jnp.max(s, axis=1, keepdims=True))
        alpha = jnp.exp(m - m_new)
        p = jnp.exp(s - m_new)
        l = alpha * l + jnp.sum(p, axis=1, keepdims=True)
        acc = alpha * acc + _dot(p.astype(BF16), v)
        return m_new, l, acc

    init = (jnp.full((2 * tq, 1), MASK_VALUE, F32), jnp.zeros((2 * tq, 1), F32),
            jnp.zeros((2 * tq, LANES), F32))
    carry = lax.fori_loop(0, qi, lambda kb, c: step(kb, c, False), init)
    _, l, acc = step(qi, carry, True)
    o = acc / l
    od = o[:tq] - lam * o[tq:]
    o_ref[...] = (_rms(od, g_ref[...]) * (1.0 - lam_init)).astype(o_ref.dtype)


def _diff_attn(lam_p, g_subln, q, k, v, *, b, s, tq, lam_init):
    nq = s // tq
    return pl.pallas_call(
        functools.partial(_diff_attn_kernel, tq=tq, lam_init=lam_init),
        grid=(b, H_DIFF, nq),
        in_specs=[pl.BlockSpec((4, HD_DIFF), lambda bi, h, qi: (0, 0)),
                  pl.BlockSpec((1, VD_DIFF), lambda bi, h, qi: (0, 0)),
                  pl.BlockSpec((tq, LANES), lambda bi, h, qi: (bi * nq + qi, h)),
                  pl.BlockSpec((s, LANES), lambda bi, h, qi: (bi, h)),
                  pl.BlockSpec((s, LANES), lambda bi, h, qi: (bi, h))],
        out_specs=pl.BlockSpec((tq, LANES), lambda bi, h, qi: (bi * nq + qi, h)),
        out_shape=jax.ShapeDtypeStruct((b * s, W_DIFF), BF16),
        compiler_params=_cparams(("arbitrary", "arbitrary", "arbitrary")),
        name="diff_attn",
    )(lam_p, g_subln.reshape(1, VD_DIFF), q, k, v)


def _sb_attn_kernel(q_ref, k_ref, v_ref, o_ref, *, tq):
    qi = pl.program_id(2)
    q = q_ref[...]
    lane = lax.broadcasted_iota(I32, (tq, LANES), 1)
    r = lax.broadcasted_iota(I32, (tq, tq), 0)
    c = lax.broadcasted_iota(I32, (tq, tq), 1)
    strict = c < r
    later = (r > c).astype(BF16)
    zero = jnp.zeros_like(q)
    qh = [jnp.where(lane < HD_SB, q, zero), jnp.where(lane >= HD_SB, q, zero)]

    def step(kb, runs, accs, masked):
        off = pl.multiple_of(kb * tq, tq)
        k = k_ref[pl.ds(off, tq), :]
        v = v_ref[pl.ds(off, tq), :]
        new_runs, new_accs = [], []
        for hh in range(2):
            z = _dot_nt(qh[hh], k)
            lsig, lom = _log_sigmoid_pair(z)
            if masked:
                lom = jnp.where(strict, lom, 0.0)
            hi, lo = _split(lom)
            after = _dot(hi, later) + _dot(lo, later)
            w = jnp.exp(lsig + after + runs[hh])
            if masked:
                w = jnp.where(strict, w, 0.0)
            new_accs.append(accs[hh] + _dot(w.astype(BF16), v))
            new_runs.append(runs[hh] + jnp.sum(lom, axis=1, keepdims=True))
        return new_runs, new_accs

    runs, accs = step(qi, [jnp.zeros((tq, 1), F32)] * 2, [jnp.zeros((tq, LANES), F32)] * 2, True)

    def cond(c):
        kb, r0, r1 = c[0], c[1], c[2]
        return (kb >= 0) & (jnp.max(jnp.maximum(r0, r1)) > EXP_UNDERFLOW)

    def body(c):
        kb, r0, r1, a0, a1 = c
        (r0, r1), (a0, a1) = step(kb, [r0, r1], [a0, a1], False)
        return kb - 1, r0, r1, a0, a1

    _, _, _, a0, a1 = lax.while_loop(cond, body, (qi - 1, runs[0], runs[1], accs[0], accs[1]))
    o_ref[...] = jnp.where(lane < HD_SB, a0, a1).astype(o_ref.dtype)


def _sb_attn(q, k, v, *, b, s, tq):
    nq = s // tq
    return pl.pallas_call(
        functools.partial(_sb_attn_kernel, tq=tq),
        grid=(b, H_SB // 2, nq),
        in_specs=[pl.BlockSpec((tq, LANES), lambda bi, h, qi: (bi * nq + qi, h)),
                  pl.BlockSpec((s, LANES), lambda bi, h, qi: (bi, h)),
                  pl.BlockSpec((s, LANES), lambda bi, h, qi: (bi, h))],
        out_specs=pl.BlockSpec((tq, LANES), lambda bi, h, qi: (bi * nq + qi, h)),
        out_shape=jax.ShapeDtypeStruct((b * s, W_SB), BF16),
        compiler_params=_cparams(("arbitrary", "arbitrary", "arbitrary")),
        name="sb_attn",
    )(q, k, v)


def _sample_attn_kernel(pt_ref, lam_ref, g_ref, qd_ref, qs_ref, kn_ref, vn_ref,
                        ckd_ref, cvd_ref, cks_ref, cvs_ref, o_ref,
                        qdr_scr, qsr_scr, m_scr, l_scr, accd_scr, run_scr, accs_scr, *, n_pages, lam_init):
    del pt_ref
    p = pl.program_id(1)
    nr = 2 * H_DIFF
    w = W_DIFF

    @pl.when(p == 0)
    def _():
        qd = qd_ref[...]
        qdr_scr[...] = jnp.transpose(jnp.broadcast_to(qd, (PAGE, w)))
        qsr_scr[...] = jnp.transpose(jnp.broadcast_to(qs_ref[...], (PAGE, w)))
        row = lax.broadcasted_iota(I32, (nr, w), 0)
        lane = lax.broadcasted_iota(I32, (nr, w), 1)
        s_self = jnp.sum(jnp.where(jnp.right_shift(lane, 6) == row, qd * kn_ref[...], 0.0), axis=1, keepdims=True)
        m_scr[...] = s_self
        l_scr[...] = jnp.ones_like(l_scr)
        accd_scr[...] = jnp.broadcast_to(vn_ref[...], accd_scr.shape)
        run_scr[...] = jnp.zeros_like(run_scr)
        accs_scr[...] = jnp.zeros_like(accs_scr)

    sd = jnp.sum((ckd_ref[...] * qdr_scr[...]).reshape(nr, HD_DIFF, PAGE), axis=1)
    m_old = m_scr[...]
    m_new = jnp.maximum(m_old, jnp.max(sd, axis=1, keepdims=True))
    alpha = jnp.exp(m_old - m_new)
    pw = jnp.exp(sd - m_new)
    m_scr[...] = m_new
    l_scr[...] = alpha * l_scr[...] + jnp.sum(pw, axis=1, keepdims=True)
    pw_hi, pw_lo = _split(pw)
    for h in range(H_DIFF):
        v_hi, v_lo = _split(cvd_ref[:, h, :])
        cols = slice(h * VD_DIFF, (h + 1) * VD_DIFF)
        accd_scr[:, cols] = alpha * accd_scr[:, cols] + (
            _dot(pw_hi, v_hi) + (_dot(pw_hi, v_lo) + _dot(pw_lo, v_hi)))

    z = jnp.sum((cks_ref[...] * qsr_scr[...]).reshape(H_SB, HD_SB, PAGE), axis=1)
    lsig, lom = _log_sigmoid_pair(z)
    t = lax.broadcasted_iota(I32, (PAGE, PAGE), 0)
    u = lax.broadcasted_iota(I32, (PAGE, PAGE), 1)
    later = (t > u).astype(BF16)
    hi, mid, lo = _split3(lom)
    after = _dot(hi, later) + (_dot(mid, later) + _dot(lo, later))
    wgt = jnp.exp(lsig + after + run_scr[...])
    wrep = jnp.broadcast_to(wgt[:, None, :], (H_SB, HD_SB, PAGE)).reshape(w, PAGE)
    accs_scr[...] = accs_scr[...] + wrep * cvs_ref[...]
    run_scr[...] = run_scr[...] + jnp.sum(lom, axis=1, keepdims=True)

    @pl.when(p == n_pages - 1)
    def _():
        lam = _lam(lam_ref[...], lam_init)
        row = lax.broadcasted_iota(I32, (nr, w), 0)
        lane = lax.broadcasted_iota(I32, (nr, w), 1)
        o = accd_scr[...] / l_scr[...]
        o = jnp.where(jnp.bitwise_and(row, 1) == 0, o, -lam * o)
        o = jnp.where(jnp.right_shift(lane, 7) == jnp.right_shift(row, 1), o, 0.0)
        od = jnp.sum(o, axis=0, keepdims=True)
        row8 = lax.broadcasted_iota(I32, (H_DIFF, w), 0)
        head8 = jnp.right_shift(lax.broadcasted_iota(I32, (H_DIFF, w), 1), 7) == row8
        ss = jnp.sum(jnp.where(head8, od * od, 0.0), axis=1, keepdims=True) * (1.0 / VD_DIFF)
        inv = jnp.sum(jnp.where(head8, lax.rsqrt(ss + EPS), 0.0), axis=0, keepdims=True)
        o_ref[:, :w] = od * inv * g_ref[...] * (1.0 - lam_init)
        hi, mid, lo = _split3(accs_scr[...])
        ones = jnp.ones((8, PAGE), BF16)
        osb = _dot_nt(ones, hi) + (_dot_nt(ones, mid) + _dot_nt(ones, lo))
        o_ref[:, w:] = osb[0:1]


def _sample_attn(page_table, lam_p, g_subln, qd, qs, kn, vn, ckd, cvd, cks, cvs, *, lam_init):
    db, n_pages = page_table.shape
    w = W_DIFF
    row3 = lambda a: a.reshape(db, 1, a.shape[-1])
    row_spec = pl.BlockSpec((None, 1, w), lambda b, p, pt: (b, 0, 0))
    fm_spec = pl.BlockSpec((None, w, PAGE), lambda b, p, pt: (pt[b, n_pages - 1 - p], 0, 0))
    tm_spec = pl.BlockSpec((None, PAGE, H_DIFF, VD_DIFF), lambda b, p, pt: (pt[b, n_pages - 1 - p], 0, 0, 0))
    nr = 2 * H_DIFF
    out = pl.pallas_call(
        functools.partial(_sample_attn_kernel, n_pages=n_pages, lam_init=lam_init),
        grid_spec=pltpu.PrefetchScalarGridSpec(
            num_scalar_prefetch=1,
            grid=(db, n_pages),
            in_specs=[pl.BlockSpec((4, HD_DIFF), lambda b, p, pt: (0, 0)),
                      pl.BlockSpec((1, w), lambda b, p, pt: (0, 0)),
                      row_spec, row_spec, row_spec, row_spec,
                      fm_spec, tm_spec, fm_spec, fm_spec],
            out_specs=pl.BlockSpec((None, 1, 2 * w), lambda b, p, pt: (b, 0, 0)),
            scratch_shapes=[pltpu.VMEM((w, PAGE), F32), pltpu.VMEM((w, PAGE), F32),
                            pltpu.VMEM((nr, 1), F32), pltpu.VMEM((nr, 1), F32), pltpu.VMEM((nr, w), F32),
                            pltpu.VMEM((H_SB, 1), F32), pltpu.VMEM((w, PAGE), F32)]),
        out_shape=jax.ShapeDtypeStruct((db, 1, 2 * w), F32),
        compiler_params=_cparams(("arbitrary", "arbitrary")),
        name="sample_attn",
    )(page_table, lam_p, jnp.tile(g_subln, H_DIFF).reshape(1, w), row3(qd), row3(qs), row3(kn), row3(vn),
      ckd, cvd, cks, cvs)
    return out.reshape(db, 2 * w)


def _route(lg):
    lane = lax.broadcasted_iota(I32, lg.shape, 1)
    lanef = lane.astype(F32)
    big = float(ROUTER_LANES)
    gmask = lane < N_GROUPS
    gmax = jnp.max(jnp.where(gmask, lg, MASK_VALUE), axis=1, keepdims=True)
    gidx = jnp.min(jnp.where(gmask & (lg == gmax), lanef, big), axis=1, keepdims=True).astype(I32)
    gsum = jnp.sum(jnp.where(gmask, jnp.exp(lg - gmax), 0.0), axis=1, keepdims=True)
    gw = 1.0 / gsum
    emask = (lane >= N_GROUPS) & (lane < N_GROUPS + N_EXPERTS) & (jnp.right_shift(lane - N_GROUPS, 3) == gidx)
    v1 = jnp.max(jnp.where(emask, lg, MASK_VALUE), axis=1, keepdims=True)
    i1 = jnp.min(jnp.where(emask & (lg == v1), lanef, big), axis=1, keepdims=True)
    emask2 = emask & (lanef != i1)
    v2 = jnp.max(jnp.where(emask2, lg, MASK_VALUE), axis=1, keepdims=True)
    i2 = jnp.min(jnp.where(emask2 & (lg == v2), lanef, big), axis=1, keepdims=True)
    e2 = jnp.exp(v2 - v1)
    t1 = 1.0 / (1.0 + e2)
    t2 = e2 / (1.0 + e2)
    comb = gw * (jnp.where(lanef == i1, t1, 0.0) + jnp.where(lanef == i2, t2, 0.0))
    return jnp.where(lane == 0, gidx.astype(F32), comb)


def _oproj_kernel(*refs, n_mix, nk):
    mix_refs = refs[:n_mix]
    (w_ref, x_ref, gpost_ref, g1_ref, gpre2_ref, sc2_ref, sh2_ref, wr_ref, br_ref,
     x1_ref, h2_ref, comb_ref, acc_scr) = refs[n_mix:]
    k = pl.program_id(1)
    hi_prec = w_ref.dtype == F32

    def mm(a, w):
        return _dot3(a, w) if hi_prec else _dot(a.astype(BF16), w)

    @pl.when(k == 0)
    def _():
        acc_scr[...] = jnp.zeros_like(acc_scr)

    if n_mix == 1:
        acc_scr[...] += mm(mix_refs[0][...], w_ref[...])
    else:
        for idx in range(n_mix):
            @pl.when(k == idx)
            def _(idx=idx):
                acc_scr[...] += mm(mix_refs[idx][...], w_ref[...])

    @pl.when(k == nk - 1)
    def _():
        x1 = x_ref[...] + g1_ref[...] * _rms(acc_scr[...], gpost_ref[...])
        h2 = _rms(x1, gpre2_ref[...]) * (1.0 + sc2_ref[...]) + sh2_ref[...]
        x1_ref[...] = x1
        h2_ref[...] = h2.astype(BF16)
        comb_ref[...] = _route(mm(h2, wr_ref[...]) + br_ref[...])


def _oproj(mixes, w, x, gpost, g1, gpre2, sc2, sh2, wr, br, *, tm, tk, rows_per_mod):
    t, d = x.shape
    n_mix = len(mixes)
    nk = w.shape[0] // tk
    if rows_per_mod is None:
        mod_spec = pl.BlockSpec((tm, d), lambda i, k: (i, 0))
    else:
        per = rows_per_mod // tm
        mod_spec = pl.BlockSpec((None, 1, d), lambda i, k: (i // per, 0, 0))
    if n_mix == 1:
        mix_specs = [pl.BlockSpec((tm, tk), lambda i, k: (i, k))]
    else:
        mix_specs = [pl.BlockSpec((tm, tk), lambda i, k: (i, 0))] * n_mix
    vec = pl.BlockSpec((1, d), lambda i, k: (0, 0))
    row = pl.BlockSpec((tm, d), lambda i, k: (i, 0))
    return pl.pallas_call(
        functools.partial(_oproj_kernel, n_mix=n_mix, nk=nk),
        grid=(t // tm, nk),
        in_specs=mix_specs + [pl.BlockSpec((tk, d), lambda i, k: (k, 0)), row, vec, mod_spec, vec,
                              mod_spec, mod_spec,
                              pl.BlockSpec((d, ROUTER_LANES), lambda i, k: (0, 0)),
                              pl.BlockSpec((1, ROUTER_LANES), lambda i, k: (0, 0))],
        out_specs=[row, row, pl.BlockSpec((tm, ROUTER_LANES), lambda i, k: (i, 0))],
        out_shape=[jax.ShapeDtypeStruct((t, d), F32), jax.ShapeDtypeStruct((t, d), BF16),
                   jax.ShapeDtypeStruct((t, ROUTER_LANES), F32)],
        scratch_shapes=[pltpu.VMEM((tm, d), F32)],
        compiler_params=_cparams(("arbitrary", "arbitrary")),
        name="oproj",
    )(*mixes, w, x, gpost.reshape(1, d), g1, gpre2.reshape(1, d), sc2, sh2, wr, br)


def _moe_kernel(first_ref, active_ref, h_ref, comb_ref, wg_ref, wu_ref, wd_ref, f_ref, *, n_e):
    i = pl.program_id(0)
    e = pl.program_id(1)

    @pl.when(e == 0)
    def _():
        f_ref[...] = jnp.zeros_like(f_ref)

    @pl.when(active_ref[i] == 1)
    def _():
        h = h_ref[...]
        act = _silu(_dot(h, wg_ref[...])) * _dot(h, wu_ref[...])
        comb = comb_ref[...]
        lane = lax.broadcasted_iota(I32, comb.shape, 1)
        ce = jnp.sum(jnp.where(lane == first_ref[i] + e + N_GROUPS, comb, 0.0), axis=1, keepdims=True)
        f_ref[...] += ce * _dot(act.astype(BF16), wd_ref[...])


def _moe(first, active, h2, comb, wg, wu, wd, *, tm, n_e):
    t, d = h2.shape
    de = wg.shape[2]
    row = pl.BlockSpec((tm, d), lambda i, e, first, active: (i, 0))
    wspec = lambda r, c: pl.BlockSpec((None, r, c), lambda i, e, first, active: (first[i] + e, 0, 0))
    return pl.pallas_call(
        functools.partial(_moe_kernel, n_e=n_e),
        grid_spec=pltpu.PrefetchScalarGridSpec(
            num_scalar_prefetch=2,
            grid=(t // tm, n_e),
            in_specs=[row, pl.BlockSpec((tm, ROUTER_LANES), lambda i, e, first, active: (i, 0)),
                      wspec(d, de), wspec(d, de), wspec(de, d)],
            out_specs=row),
        out_shape=jax.ShapeDtypeStruct((t, d), F32),
        compiler_params=_cparams(("arbitrary", "arbitrary")),
        name="moe",
    )(first, active, h2, comb, wg, wu, wd)


def _moe_out_kernel(x1_ref, f_ref, g2_ref, gpost_ref, y_ref):
    y_ref[...] = x1_ref[...] + g2_ref[...] * _rms(f_ref[...], gpost_ref[...])


def _moe_out(x1, f, g2, gpost, *, tm, rows_per_mod):
    t, d = x1.shape
    if rows_per_mod is None:
        mod_spec = pl.BlockSpec((tm, d), lambda i: (i, 0))
    else:
        per = rows_per_mod // tm
        mod_spec = pl.BlockSpec((None, 1, d), lambda i: (i // per, 0, 0))
    row = pl.BlockSpec((tm, d), lambda i: (i, 0))
    return pl.pallas_call(
        _moe_out_kernel,
        grid=(t // tm,),
        in_specs=[row, row, mod_spec, pl.BlockSpec((1, d), lambda i: (0, 0))],
        out_specs=row,
        out_shape=jax.ShapeDtypeStruct((t, d), F32),
        compiler_params=_cparams(("arbitrary",)),
        name="moe_out",
    )(x1, f, g2, gpost.reshape(1, d))


def _group_dispatch(comb, tm):
    t = comb.shape[0]
    t_pad = t + N_GROUPS * tm
    gid = comb[:, 0].astype(I32)
    order = jnp.argsort(gid, stable=True).astype(I32)
    counts = jnp.sum(gid[:, None] == jnp.arange(N_GROUPS, dtype=I32)[None, :], axis=0, dtype=I32)
    padded = (counts + tm - 1) // tm * tm
    ends_padded = jnp.cumsum(padded)
    starts_padded = ends_padded - padded
    starts = jnp.cumsum(counts) - counts
    gs = gid[order]
    dest = starts_padded[gs] + (jnp.arange(t, dtype=I32) - starts[gs])
    src = jnp.zeros((t_pad,), I32).at[dest].set(order)
    valid = jnp.zeros((t_pad,), jnp.bool_).at[dest].set(True)
    pos = jnp.zeros((t,), I32).at[order].set(dest)
    tile_start = jnp.arange(t_pad // tm, dtype=I32) * tm
    tile_group = jnp.minimum(jnp.sum(tile_start[:, None] >= ends_padded[None, :], axis=1, dtype=I32), N_GROUPS - 1)
    active = (tile_start < ends_padded[-1]).astype(I32)
    return src, valid, pos, tile_group * EXPERTS_PER_GROUP, active


def _moe_prompt(h2, comb, wg, wu, wd, *, tm):
    src, valid, pos, first, active = _group_dispatch(comb, tm)
    h2_s = h2[src]
    comb_s = jnp.where(valid[:, None], comb[src], 0.0)
    f_s = _moe(first, active, h2_s, comb_s, wg, wu, wd, tm=tm, n_e=EXPERTS_PER_GROUP)
    return f_s[pos]


def _tile(n, want):
    return want if n % want == 0 else n


def _feature_major(cache):
    nd = cache.ndim
    return jnp.transpose(cache, (0,) + tuple(range(2, nd)) + (1,)).reshape(cache.shape[0], W_DIFF, PAGE)


def kernel(x_prompt, x_sample, c_prompt, c_sample, cache_diff_k, cache_diff_v, cache_sb_k, cache_sb_v, page_table, w_ada, b_ada, g_pre1, w_qkv, lambda_q1, lambda_k1, lambda_q2, lambda_k2, g_subln, w_o, g_post1, g_pre2, w_router_group, b_router_group, w_router_expert, b_router_expert, w_gate, w_up, w_down, g_post2):
    b, s, d = x_prompt.shape
    db = x_sample.shape[0]
    depth = w_ada.shape[0]
    n_pages = page_table.shape[1]
    n_pool = cache_diff_k.shape[1]
    t = b * s
    xp = x_prompt.reshape(t, d)
    xs = x_sample.reshape(db, d)
    rows = b + db
    rows_pad = -(-rows // 8) * 8
    c_all = jnp.concatenate([c_prompt, c_sample, jnp.zeros((rows_pad - rows, d), F32)], axis=0)
    rope_p = _rope_tables(jnp.arange(s))
    rope_s = _rope_tables(jnp.full((db,), n_pages * PAGE))

    tm_qkv = _tile(s, 512)
    tq_diff = _tile(s, 256)
    tq_sb = _tile(s, 256)
    tm_o = _tile(s, 256)
    tm_moe = _tile(s, 512)

    outs = [[] for _ in range(8)]
    for l in range(depth):
        lam_init = 0.8 - 0.6 * math.exp(-0.3 * l)
        lam_p = jnp.stack([lambda_q1[l], lambda_k1[l], lambda_q2[l], lambda_k2[l]])
        mod = _ada(c_all, w_ada[l], b_ada[l])
        mp = [mod[:b, i * d:(i + 1) * d].reshape(b, 1, d) for i in range(6)]
        ms = [mod[b:b + db, i * d:(i + 1) * d] for i in range(6)]
        pad = ROUTER_LANES - N_GROUPS - N_EXPERTS
        wr = jnp.concatenate([w_router_group[l], w_router_expert[l].reshape(d, N_EXPERTS),
                              jnp.zeros((d, pad), F32)], axis=1)
        br = jnp.concatenate([b_router_group[l], b_router_expert[l].reshape(N_EXPERTS),
                              jnp.zeros((pad,), F32)]).reshape(1, ROUTER_LANES)
        wqkv16, wo16 = w_qkv[l].astype(BF16), w_o[l].astype(BF16)
        wg16, wu16, wd16 = w_gate[l].astype(BF16), w_up[l].astype(BF16), w_down[l].astype(BF16)

        qd, kd, kd16, vd, vd16, qs, ks, ks16, vs, vs16 = _qkv(
            xp, mp[0], mp[1], g_pre1[l], wqkv16, *rope_p, tm=tm_qkv, rows_per_mod=s, rope_rows=s, with_bf16=True)
        mix_d = _diff_attn(lam_p, g_subln[l], qd, kd16, vd16, b=b, s=s, tq=tq_diff, lam_init=lam_init)
        mix_s = _sb_attn(qs, ks16, vs16, b=b, s=s, tq=tq_sb)
        x1, h2, comb = _oproj([mix_d, mix_s], wo16, xp, g_post1[l], mp[2], g_pre2[l], mp[4], mp[3],
                              wr.astype(BF16), br, tm=tm_o, tk=W_DIFF, rows_per_mod=s)
        f = _moe_prompt(h2, comb, wg16, wu16, wd16, tm=tm_moe)
        xp = _moe_out(x1, f, mp[5], g_post2[l], tm=tm_moe, rows_per_mod=s)
        for lst, a in zip(outs[:4], (kd, vd, ks, vs)):
            lst.append(a)

        qd, kd, vd, qs, ks, vs = _qkv(xs, ms[0], ms[1], g_pre1[l], w_qkv[l], *rope_s,
                                      tm=db, rows_per_mod=None, rope_rows=db, with_bf16=False)
        mix = _sample_attn(page_table, lam_p, g_subln[l], qd, qs, kd, vd,
                           _feature_major(cache_diff_k[l]), cache_diff_v[l],
                           _feature_major(cache_sb_k[l]), _feature_major(cache_sb_v[l]), lam_init=lam_init)
        x1, h2, comb = _oproj([mix], w_o[l], xs, g_post1[l], ms[2], g_pre2[l], ms[4], ms[3], wr, br,
                              tm=db, tk=512, rows_per_mod=None)
        f = _moe(jnp.zeros((1,), I32), jnp.ones((1,), I32), h2, comb, wg16, wu16, wd16, tm=db, n_e=N_EXPERTS)
        xs = _moe_out(x1, f, ms[5], g_post2[l], tm=db, rows_per_mod=None)
        for lst, a in zip(outs[4:], (kd, vd, ks, vs)):
            lst.append(a)

    def stack(lst, lead, tail):
        return jnp.stack(lst).reshape((depth,) + lead + tail)

    return (xp.reshape(b, s, d), xs.reshape(db, 1, d),
            stack(outs[0], (b, s), (H_DIFF, 2, HD_DIFF)), stack(outs[1], (b, s), (H_DIFF, VD_DIFF)),
            stack(outs[2], (b, s), (H_SB, HD_SB)), stack(outs[3], (b, s), (H_SB, HD_SB)),
            stack(outs[4], (db, 1), (H_DIFF, 2, HD_DIFF)), stack(outs[5], (db, 1), (H_DIFF, VD_DIFF)),
            stack(outs[6], (db, 1), (H_SB, HD_SB)), stack(outs[7], (db, 1), (H_SB, HD_SB)))
```
